```python
import jax
import jax.numpy as jnp
from jax import lax
import numpy as np

D_MODEL = 4096
BATCH = 1
SEQ = 16384
DEPTH = 4

CTX_LEN = 256
GRID_W = 64
N_MIXERS = 3
POOL_WINDOWS = (2, 4, 8, 16)
N_POOL_GROUPS = 4
POOL_GROUP_DIM = D_MODEL // N_POOL_GROUPS
N_HEADS = 32
HEAD_DIM = D_MODEL // N_HEADS
NA_KH = 8
NA_KW = 16
CONV_WIDTH = 31
N_EXPERTS = 16
EC_CAPACITY = 2
D_EXPERT = (D_MODEL * 3) // 32
EPS = 1e-6
N_POOL_LAYERS = (DEPTH + 2) // 3
N_ATTN_LAYERS = (DEPTH + 1) // 3
N_CONV_LAYERS = DEPTH // 3

kernel_name = 'hybrid_pool_natten_conformer_ecmoe_dit'


def _rmsnorm(x, g):
    xf = x.astype(jnp.float32)
    y = xf * lax.rsqrt(jnp.mean(xf * xf, axis=-1, keepdims=True) + EPS)
    return (y * g.astype(jnp.float32)).astype(x.dtype)


def _modulate(h, shift, scale):
    return h * (1 + scale) + shift


def _ada_mods(cond, w, b):
    return jnp.split(jax.nn.silu(cond) @ w + b, 6, axis=-1)


def _pool_mixer(h, w_grp, scale):
    b, n, d = h.shape
    hf = h.astype(jnp.float32)
    cs = jnp.concatenate([jnp.zeros((b, 1, d), jnp.float32), jnp.cumsum(hf, axis=1)], axis=1)
    cs = cs.reshape(b, n + 1, N_POOL_GROUPS, POOL_GROUP_DIM)
    t = jnp.arange(n)[None, :]
    win = jnp.array(POOL_WINDOWS, jnp.int32)[:, None]
    lo = jnp.clip(t - win // 2, 0, n)
    hi = jnp.clip(t - win // 2 + win, 0, n)
    gi = jnp.arange(N_POOL_GROUPS)[:, None]
    sums = cs[:, hi, gi] - cs[:, lo, gi]
    mean = sums / (hi - lo).astype(jnp.float32)[None, :, :, None]
    diff = jnp.swapaxes(mean, 1, 2) - hf.reshape(b, n, N_POOL_GROUPS, POOL_GROUP_DIM)
    y = jnp.einsum('bngc,gcd->bngd', diff.astype(h.dtype), w_grp).reshape(b, n, d)
    return y * scale


def _na_attention(h, hc, w_qkv, w_o, q_g, k_g, rpb, update_ctx):
    b, n, d = h.shape
    lc = hc.shape[1]
    rows = n // GRID_W
    kh = min(NA_KH, rows)
    scale = HEAD_DIM ** -0.5
    qkv = (h @ w_qkv).reshape(b, n, 3, N_HEADS, HEAD_DIM)
    q = _rmsnorm(qkv[:, :, 0], q_g) * scale
    k = _rmsnorm(qkv[:, :, 1], k_g)
    v = qkv[:, :, 2]
    kvc = (hc @ w_qkv[:, d:]).reshape(b, lc, 2, N_HEADS, HEAD_DIM)
    kc = _rmsnorm(kvc[:, :, 0], k_g)
    vc = kvc[:, :, 1]
    qg = q.reshape(b, rows, GRID_W, N_HEADS, HEAD_DIM)
    kg = k.reshape(b, rows, GRID_W, N_HEADS, HEAD_DIM)
    vg = v.reshape(b, rows, GRID_W, N_HEADS, HEAD_DIM)
    cols = jnp.arange(GRID_W)
    col_idx = jnp.clip(cols - NA_KW // 2, 0, GRID_W - NA_KW)[:, None] + jnp.arange(NA_KW)
    dc_idx = col_idx - cols[:, None] + NA_KW - 1
    n_loc = kh * NA_KW

    def row_attn(r):
        rs = jnp.clip(r - kh // 2, 0, rows - kh)
        dr_idx = rs + jnp.arange(kh) - r + NA_KH - 1
        bias = rpb[:, dr_idx[None, :, None], dc_idx[:, None, :]]
        q_r = lax.dynamic_index_in_dim(qg, r, axis=1, keepdims=False)
        k_rows = lax.dynamic_slice_in_dim(kg, rs, kh, axis=1)
        v_rows = lax.dynamic_slice_in_dim(vg, rs, kh, axis=1)
        k_win = k_rows[:, :, col_idx]
        v_win = v_rows[:, :, col_idx]
        s_loc = jnp.einsum('bqhd,brqwhd->bhqrw', q_r, k_win).astype(jnp.float32) + bias[None].astype(jnp.float32)
        s_ctx = jnp.einsum('bqhd,bchd->bhqc', q_r, kc).astype(jnp.float32)
        s = jnp.concatenate([s_loc.reshape(b, N_HEADS, GRID_W, n_loc), s_ctx], axis=-1)
        p = jax.nn.softmax(s, axis=-1).astype(v.dtype)
        p_loc = p[..., :n_loc].reshape(b, N_HEADS, GRID_W, kh, NA_KW)
        p_ctx = p[..., n_loc:]
        return (jnp.einsum('bhqrw,brqwhd->bqhd', p_loc, v_win)
                + jnp.einsum('bhqc,bchd->bqhd', p_ctx, vc))

    o = lax.map(row_attn, jnp.arange(rows))
    y = jnp.moveaxis(o, 0, 1).reshape(b, n, d) @ w_o
    yc = None
    if update_ctx:
        qc = _rmsnorm((hc @ w_qkv[:, :d]).reshape(b, lc, N_HEADS, HEAD_DIM), q_g) * scale
        sc = jnp.einsum('bqhd,bkhd->bhqk', qc, kc).astype(jnp.float32)
        pc = jax.nn.softmax(sc, axis=-1).astype(vc.dtype)
        yc = jnp.einsum('bhqk,bkhd->bqhd', pc, vc).reshape(b, lc, d) @ w_o
    return y, yc


def _conv_module(h, w_pw1, b_pw1, w_dw, b_dw, ln_g, ln_b, w_pw2, b_pw2):
    u = h @ w_pw1 + b_pw1
    a, g = jnp.split(u, 2, axis=-1)
    u = a * jax.nn.sigmoid(g)
    pad = CONV_WIDTH // 2
    u = lax.conv_general_dilated(u, w_dw[:, None, :].astype(u.dtype), window_strides=(1,),
                                 padding=[(pad, pad)], dimension_numbers=('NWC', 'WIO', 'NWC'),
                                 feature_group_count=u.shape[-1]) + b_dw
    uf = u.astype(jnp.float32)
    mu = jnp.mean(uf, axis=-1, keepdims=True)
    var = jnp.mean(jnp.square(uf - mu), axis=-1, keepdims=True)
    u = ((uf - mu) * lax.rsqrt(var + EPS) * ln_g.astype(jnp.float32) + ln_b.astype(jnp.float32)).astype(h.dtype)
    return jax.nn.silu(u) @ w_pw2 + b_pw2


def _ec_moe(h, w_router, w_gate, w_up, w_down):
    b, n, d = h.shape
    cap = max(1, EC_CAPACITY * n // N_EXPERTS)
    aff = jax.nn.softmax((h @ w_router).astype(jnp.float32), axis=-1)
    g, idx = lax.top_k(jnp.swapaxes(aff, 1, 2), cap)
    bi = jnp.arange(b)[:, None, None]
    xin = h[bi, idx]
    hid = jax.nn.silu(jnp.einsum('becd,edf->becf', xin, w_gate)) * jnp.einsum('becd,edf->becf', xin, w_up)
    y = jnp.einsum('becf,efd->becd', hid, w_down) * g[..., None].astype(h.dtype)
    return jnp.zeros_like(h).at[bi, idx].add(y)


def setup_inputs(seed: int = 0) -> dict:
    key = jax.random.key(seed)
    ks = jax.random.split(key, 26)
    D = D_MODEL

    def nrm(k, shape, s):
        return jax.random.normal(k, shape, jnp.float32) * s

    return {
        'x': nrm(ks[0], (BATCH, SEQ, D), 1.0),
        'c': nrm(ks[1], (BATCH, D), 1.0),
        'ctx': nrm(ks[2], (BATCH, CTX_LEN, D), 1.0),
        'c_ctx': nrm(ks[3], (D,), 1.0),
        'ada_w': nrm(ks[4], (DEPTH, D, 6 * D), 0.5 * D ** -0.5),
        'ada_b': nrm(ks[5], (DEPTH, 6 * D), 0.01),
        'norm_g': 1.0 + nrm(ks[6], (DEPTH, 2, D), 0.02),
        'pool_w': nrm(ks[7], (N_POOL_LAYERS, N_POOL_GROUPS, POOL_GROUP_DIM, POOL_GROUP_DIM), POOL_GROUP_DIM ** -0.5),
        'pool_scale': 1.0 + nrm(ks[8], (N_POOL_LAYERS, D), 0.1),
        'na_w_qkv': nrm(ks[9], (N_ATTN_LAYERS, D, 3 * D), D ** -0.5),
        'na_w_o': nrm(ks[10], (N_ATTN_LAYERS, D, D), D ** -0.5),
        'na_q_g': 1.0 + nrm(ks[11], (N_ATTN_LAYERS, HEAD_DIM), 0.02),
        'na_k_g': 1.0 + nrm(ks[12], (N_ATTN_LAYERS, HEAD_DIM), 0.02),
        'na_rpb': nrm(ks[13], (N_ATTN_LAYERS, N_HEADS, 2 * NA_KH - 1, 2 * NA_KW - 1), 0.1),
        'conv_w_pw1': nrm(ks[14], (N_CONV_LAYERS, D, 2 * D), D ** -0.5),
        'conv_b_pw1': nrm(ks[15], (N_CONV_LAYERS, 2 * D), 0.01),
        'conv_w_dw': nrm(ks[16], (N_CONV_LAYERS, CONV_WIDTH, D), CONV_WIDTH ** -0.5),
        'conv_b_dw': nrm(ks[17], (N_CONV_LAYERS, D), 0.01),
        'conv_ln_g': 1.0 + nrm(ks[18], (N_CONV_LAYERS, D), 0.02),
        'conv_ln_b': nrm(ks[19], (N_CONV_LAYERS, D), 0.01),
        'conv_w_pw2': nrm(ks[20], (N_CONV_LAYERS, D, D), D ** -0.5),
        'conv_b_pw2': nrm(ks[21], (N_CONV_LAYERS, D), 0.01),
        'moe_w_router': nrm(ks[22], (DEPTH, D, N_EXPERTS), D ** -0.5),
        'moe_w_gate': nrm(ks[23], (DEPTH, N_EXPERTS, D, D_EXPERT), D ** -0.5),
        'moe_w_up': nrm(ks[24], (DEPTH, N_EXPERTS, D, D_EXPERT), D ** -0.5),
        'moe_w_down': nrm(ks[25], (DEPTH, N_EXPERTS, D_EXPERT, D), D_EXPERT ** -0.5),
    }


def reference(x, c, ctx, c_ctx, ada_w, ada_b, norm_g, pool_w, pool_scale, na_w_qkv, na_w_o, na_q_g, na_k_g,
              na_rpb, conv_w_pw1, conv_b_pw1, conv_w_dw, conv_b_dw, conv_ln_g, conv_ln_b, conv_w_pw2,
              conv_b_pw2, moe_w_router, moe_w_gate, moe_w_up, moe_w_down):
    for i in range(DEPTH):
        kind = i % N_MIXERS
        slot = i // N_MIXERS
        ctx_out_live = any(j % N_MIXERS == 1 for j in range(i + 1, DEPTH))
        ctx_in_live = ctx_out_live or kind == 1

        sh1, sc1, g1, sh2, sc2, g2 = [m[:, None, :] for m in _ada_mods(c, ada_w[i], ada_b[i])]
        h = _modulate(_rmsnorm(x, norm_g[i, 0]), sh1, sc1)
        hc = None
        if ctx_in_live:
            csh1, csc1, cg1, csh2, csc2, cg2 = _ada_mods(c_ctx, ada_w[i], ada_b[i])
            hc = _modulate(_rmsnorm(ctx, norm_g[i, 0]), csh1, csc1)

        yc = None
        if kind == 0:
            y = _pool_mixer(h, pool_w[slot], pool_scale[slot])
            if ctx_out_live:
                yc = _pool_mixer(hc, pool_w[slot], pool_scale[slot])
        elif kind == 1:
            y, yc = _na_attention(h, hc, na_w_qkv[slot], na_w_o[slot], na_q_g[slot], na_k_g[slot],
                                  na_rpb[slot], ctx_out_live)
        else:
            conv_args = (conv_w_pw1[slot], conv_b_pw1[slot], conv_w_dw[slot], conv_b_dw[slot],
                         conv_ln_g[slot], conv_ln_b[slot], conv_w_pw2[slot], conv_b_pw2[slot])
            y = _conv_module(h, *conv_args)
            if ctx_out_live:
                yc = _conv_module(hc, *conv_args)
        x = x + g1 * y

        h2 = _modulate(_rmsnorm(x, norm_g[i, 1]), sh2, sc2)
        x = x + g2 * _ec_moe(h2, moe_w_router[i], moe_w_gate[i], moe_w_up[i], moe_w_down[i])
        if ctx_out_live:
            ctx = ctx + cg1 * yc
            hc2 = _modulate(_rmsnorm(ctx, norm_g[i, 1]), csh2, csc2)
            ctx = ctx + cg2 * _ec_moe(hc2, moe_w_router[i], moe_w_gate[i], moe_w_up[i], moe_w_down[i])
    return x
```

```python
import functools

import jax
import jax.numpy as jnp
from jax import lax
from jax.experimental import pallas as pl
from jax.experimental.pallas import tpu as pltpu

EPS = 1e-6
GRID_W = 64
HEAD_DIM = 128
NA_KH = 8
NA_KW = 16
Q_ROWS = 4
EC_CAPACITY = 2
POOL_WINDOWS = (2, 4, 8, 16)
LANES = 128
SUBLANES = 8
MASK_VALUE = -1e30
F32 = jnp.float32
BF16 = jnp.bfloat16


def _params(sem, vmem_mb):
    return pltpu.CompilerParams(dimension_semantics=sem, vmem_limit_bytes=vmem_mb * 1024 * 1024)


def _norm_mod(x, g, sc, sh):
    ms = jnp.mean(x * x, axis=-1, keepdims=True)
    y = x * lax.rsqrt(ms + EPS)
    return (y * g) * (1.0 + sc) + sh


def _silu(v):
    return v * jax.nn.sigmoid(v)


def _dot(a, b):
    return jnp.dot(a, b, preferred_element_type=F32)


def _dot_nt(a, b):
    return lax.dot_general(a, b, (((1,), (1,)), ((), ())), preferred_element_type=F32)


def _ada_kernel(cb_ref, w_ref, b_ref, o_ref):
    k_dim, tn = w_ref.shape[1], w_ref.shape[2]
    reps = tn // LANES

    def body(kc, accs):
        k0 = pl.multiple_of(kc * SUBLANES, SUBLANES)
        w = w_ref[0, pl.ds(k0, SUBLANES), :]
        out = []
        for v in range(2):
            s = _silu(cb_ref[v, pl.ds(k0, SUBLANES), :])
            out.append(accs[v] + w * jnp.concatenate([s] * reps, axis=1))
        return tuple(out)

    zero = jnp.zeros((SUBLANES, tn), F32)
    accs = lax.fori_loop(0, k_dim // SUBLANES, body, (zero, zero), unroll=8)
    rows = [jnp.sum(a, axis=0, keepdims=True) for a in accs]
    o_ref[0] = jnp.concatenate(rows, axis=0) + b_ref[0]


def _ada_mods(c2, ada_w, ada_b):
    n_layers, k_dim, n6 = ada_w.shape
    tn = min(512, n6)
    cb = jnp.broadcast_to(c2[:, :, None], (2, k_dim, LANES))
    return pl.pallas_call(
        _ada_kernel,
        grid=(n_layers, n6 // tn),
        in_specs=[pl.BlockSpec((2, k_dim, LANES), lambda l, j: (0, 0, 0)),
                  pl.BlockSpec((1, k_dim, tn), lambda l, j: (l, 0, j)),
                  pl.BlockSpec((1, 1, tn), lambda l, j: (l, 0, j))],
        out_specs=pl.BlockSpec((1, 2, tn), lambda l, j: (l, 0, j)),
        out_shape=jax.ShapeDtypeStruct((n_layers, 2, n6), F32),
        compiler_params=_params(("arbitrary", "arbitrary"), 48),
        name="ada_mods",
    )(cb, ada_w, ada_b.reshape(n_layers, 1, n6))


def _vec_rows(*vecs):
    d = vecs[0].shape[-1]
    rows = [v.reshape(1, d).astype(F32) for v in vecs]
    rows.append(jnp.zeros((SUBLANES - len(vecs), d), F32))
    return jnp.concatenate(rows, axis=0)


def _norm_kernel(x_ref, v_ref, o_ref):
    o_ref[...] = _norm_mod(x_ref[...], v_ref[0:1, :], v_ref[1:2, :], v_ref[2:3, :]).astype(o_ref.dtype)


def _norm_bf16(x, g, sc, sh):
    n, d = x.shape
    tm = min(256, n)
    return pl.pallas_call(
        _norm_kernel,
        grid=(n // tm,),
        in_specs=[pl.BlockSpec((tm, d), lambda i: (i, 0)),
                  pl.BlockSpec((SUBLANES, d), lambda i: (0, 0))],
        out_specs=pl.BlockSpec((tm, d), lambda i: (i, 0)),
        out_shape=jax.ShapeDtypeStruct((n, d), BF16),
        compiler_params=_params(("arbitrary",), 48),
        name="norm_mod",
    )(x, _vec_rows(g, sc, sh))


def _pool_kernel(xp_ref, x_ref, xn_ref, v_ref, w_ref, o_ref, hp_ref, *, n_total):
    i = pl.program_id(0)
    nt = pl.num_programs(0)
    tm, d = x_ref.shape
    halo = SUBLANES
    g, sc, sh = v_ref[0:1, :], v_ref[1:2, :], v_ref[2:3, :]
    x = x_ref[...]
    h = _norm_mod(x, g, sc, sh)
    hp_ref[halo:halo + tm, :] = h
    hp_ref[0:halo, :] = jnp.where(i > 0, _norm_mod(xp_ref[...], g, sc, sh), 0.0)
    hp_ref[halo + tm:, :] = jnp.where(i < nt - 1, _norm_mod(xn_ref[...], g, sc, sh), 0.0)
    t = i * tm + lax.broadcasted_iota(jnp.int32, (tm, 1), 0)
    dg = d // len(POOL_WINDOWS)
    for gi, win in enumerate(POOL_WINDOWS):
        c0 = gi * dg
        acc = None
        for off in range(-(win // 2), win // 2):
            part = hp_ref[halo + off:halo + off + tm, c0:c0 + dg]
            acc = part if acc is None else acc + part
        cnt = jnp.minimum(t - win // 2 + win, n_total) - jnp.maximum(t - win // 2, 0)
        diff = acc / cnt.astype(F32) - h[:, c0:c0 + dg]
        y = _dot(diff.astype(BF16), w_ref[gi]) * v_ref[4:5, c0:c0 + dg]
        o_ref[:, c0:c0 + dg] = x[:, c0:c0 + dg] + v_ref[3:4, c0:c0 + dg] * y


def _pool_layer(x, g, sc, sh, gate, w_grp, pool_scale):
    n, d = x.shape
    tm = min(256, n)
    hb = tm // SUBLANES
    nb8 = n // SUBLANES
    n_groups, dg, _ = w_grp.shape
    return pl.pallas_call(
        functools.partial(_pool_kernel, n_total=n),
        grid=(n // tm,),
        in_specs=[pl.BlockSpec((SUBLANES, d), lambda i: (jnp.maximum(i * hb - 1, 0), 0)),
                  pl.BlockSpec((tm, d), lambda i: (i, 0)),
                  pl.BlockSpec((SUBLANES, d), lambda i: (jnp.minimum((i + 1) * hb, nb8 - 1), 0)),
                  pl.BlockSpec((SUBLANES, d), lambda i: (0, 0)),
                  pl.BlockSpec((n_groups, dg, dg), lambda i: (0, 0, 0))],
        out_specs=pl.BlockSpec((tm, d), lambda i: (i, 0)),
        out_shape=jax.ShapeDtypeStruct((n, d), F32),
        scratch_shapes=[pltpu.VMEM((tm + 2 * SUBLANES, d), F32)],
        compiler_params=_params(("arbitrary",), 56),
        name="pool_layer",
    )(x, x, x, _vec_rows(g, sc, sh, gate, pool_scale), w_grp.astype(BF16))


def _router_kernel(x_ref, v_ref, whi_ref, wlo_ref, aff_ref, xc_ref, *, n_experts):
    x = x_ref[...]
    xc_ref[...] = x
    h = _norm_mod(x, v_ref[0:1, :], v_ref[1:2, :], v_ref[2:3, :])
    h_hi = h.astype(BF16)
    h_lo = (h - h_hi.astype(F32)).astype(BF16)
    logits = _dot(h_hi, whi_ref[...]) + (_dot(h_hi, wlo_ref[...]) + _dot(h_lo, whi_ref[...]))
    lt = logits.T[0:n_experts, :]
    m = jnp.max(lt, axis=0, keepdims=True)
    p = jnp.exp(lt - m)
    aff_ref[...] = p / jnp.sum(p, axis=0, keepdims=True)


def _router(x, g, sc, sh, w_router):
    n, d = x.shape
    n_experts = w_router.shape[1]
    tm = min(256, n)
    wpad = jnp.zeros((d, LANES), F32).at[:, :n_experts].set(w_router)
    w_hi = wpad.astype(BF16)
    w_lo = (wpad - w_hi.astype(F32)).astype(BF16)
    return pl.pallas_call(
        functools.partial(_router_kernel, n_experts=n_experts),
        grid=(n // tm,),
        in_specs=[pl.BlockSpec((tm, d), lambda i: (i, 0)),
                  pl.BlockSpec((SUBLANES, d), lambda i: (0, 0)),
                  pl.BlockSpec((d, LANES), lambda i: (0, 0)),
                  pl.BlockSpec((d, LANES), lambda i: (0, 0))],
        out_specs=[pl.BlockSpec((n_experts, tm), lambda i: (0, i)),
                   pl.BlockSpec((tm, d), lambda i: (i, 0))],
        out_shape=[jax.ShapeDtypeStruct((n_experts, n), F32),
                   jax.ShapeDtypeStruct((n, d), F32)],
        compiler_params=_params(("arbitrary",), 48),
        name="router",
    )(x, _vec_rows(g, sc, sh), w_hi, w_lo)


def _moe_kernel(idx_ref, g_ref, v_ref, wg_ref, wu_ref, wd_ref, xin_hbm, xacc_hbm, o_hbm,
                hbuf, xbuf, sem_h, sem_x, sem_o, pend):
    del xacc_hbm
    e, s = pl.program_id(0), pl.program_id(1)
    ns, ne = pl.num_programs(1), pl.num_programs(0)
    st = e * ns + s
    slot = lax.rem(st, 2)
    other = 1 - slot
    ts = hbuf.shape[1]
    base = st * ts

    def gather(src, dst, sem, first):
        def body(r, c):
            row = idx_ref[first + r]
            pltpu.make_async_copy(src.at[pl.ds(row, 1)], dst.at[pl.ds(r, 1)], sem).start()
            return c
        lax.fori_loop(0, ts, body, 0)

    def scatter(src, sem, first):
        def body(r, c):
            row = idx_ref[first + r]
            pltpu.make_async_copy(src.at[pl.ds(r, 1)], o_hbm.at[pl.ds(row, 1)], sem).start()
            return c
        lax.fori_loop(0, ts, body, 0)

    def wait_rows(buf, sem):
        def body(r, c):
            pltpu.make_async_copy(o_hbm.at[pl.ds(0, 1)], buf.at[pl.ds(0, 1)], sem).wait()
            return c
        lax.fori_loop(0, ts, body, 0)

    @pl.when(st == 0)
    def _():
        pend[0] = 0
        pend[1] = 0
        gather(xin_hbm, hbuf.at[0], sem_h.at[0], 0)

    @pl.when(pend[slot] == 1)
    def _():
        wait_rows(xbuf.at[slot], sem_o.at[slot])
        pend[slot] = 0

    @pl.when(jnp.logical_and(s == 0, pend[other] == 1))
    def _():
        wait_rows(xbuf.at[other], sem_o.at[other])
        pend[other] = 0

    gather(o_hbm, xbuf.at[slot], sem_x.at[slot], base)

    @pl.when(st + 1 < ne * ns)
    def _():
        gather(xin_hbm, hbuf.at[other], sem_h.at[other], base + ts)

    wait_rows(hbuf.at[slot], sem_h.at[slot])
    h = _norm_mod(hbuf[slot], v_ref[0:1, :], v_ref[1:2, :], v_ref[2:3, :]).astype(BF16)
    a = _dot(h, wg_ref[0])
    b = _dot(h, wu_ref[0])
    hid = (_silu(a) * b).astype(BF16)
    y = _dot(hid, wd_ref[0]) * g_ref[0]
    wait_rows(xbuf.at[slot], sem_x.at[slot])
    xbuf[slot] = xbuf[slot] + v_ref[3:4, :] * y
    scatter(xbuf.at[slot], sem_o.at[slot], base)
    pend[slot] = 1

    @pl.when(st == ne * ns - 1)
    def _():
        wait_rows(xbuf.at[slot], sem_o.at[slot])
        pend[slot] = 0

        @pl.when(pend[other] == 1)
        def _():
            wait_rows(xbuf.at[other], sem_o.at[other])
            pend[other] = 0


def _moe_apply(x_in, x_acc, idx, gates, g, sc, sh, gate2, w_gate, w_up, w_down):
    n, d = x_in.shape
    n_experts, cap = idx.shape
    f = w_gate.shape[-1]
    ts = min(256, cap)
    grid_spec = pltpu.PrefetchScalarGridSpec(
        num_scalar_prefetch=1,
        grid=(n_experts, cap // ts),
        in_specs=[pl.BlockSpec((1, ts, 1), lambda e, s, idx_ref: (e, s, 0)),
                  pl.BlockSpec((SUBLANES, d), lambda e, s, idx_ref: (0, 0)),
                  pl.BlockSpec((1, d, f), lambda e, s, idx_ref: (e, 0, 0)),
                  pl.BlockSpec((1, d, f), lambda e, s, idx_ref: (e, 0, 0)),
                  pl.BlockSpec((1, f, d), lambda e, s, idx_ref: (e, 0, 0)),
                  pl.BlockSpec(memory_space=pl.ANY),
                  pl.BlockSpec(memory_space=pl.ANY)],
        out_specs=pl.BlockSpec(memory_space=pl.ANY),
        scratch_shapes=[pltpu.VMEM((2, ts, d), F32),
                        pltpu.VMEM((2, ts, d), F32),
                        pltpu.SemaphoreType.DMA((2,)),
                        pltpu.SemaphoreType.DMA((2,)),
                        pltpu.SemaphoreType.DMA((2,)),
                        pltpu.SMEM((2,), jnp.int32)])
    return pl.pallas_call(
        _moe_kernel,
        grid_spec=grid_spec,
        out_shape=jax.ShapeDtypeStruct((n, d), F32),
        input_output_aliases={7: 0},
        compiler_params=_params(("arbitrary", "arbitrary"), 56),
        name="moe_experts",
    )(idx.reshape(-1).astype(jnp.int32), gates.reshape(n_experts, cap, 1), _vec_rows(g, sc, sh, gate2),
      w_gate, w_up, w_down, x_in, x_acc)


def _moe_layer(x, g, sc, sh, gate2, w_router, w_gate, w_up, w_down):
    n = x.shape[0]
    n_experts = w_router.shape[1]
    cap = max(1, EC_CAPACITY * n // n_experts)
    aff_t, x_acc = _router(x, g, sc, sh, w_router)
    gates, idx = lax.top_k(aff_t, cap)
    return _moe_apply(x, x_acc, idx, gates, g, sc, sh, gate2, w_gate, w_up, w_down)


def _mm_residual_kernel(a_ref, w_ref, b_ref, gate_ref, x_ref, o_ref):
    y = _dot(a_ref[...], w_ref[...]) + b_ref[...]
    o_ref[...] = x_ref[...] + gate_ref[...] * y


def _mm_residual(a, w, bias, gate, x):
    m, k_dim = a.shape
    n = w.shape[1]
    tm, tn = min(512, m), min(1024, n)
    return pl.pallas_call(
        _mm_residual_kernel,
        grid=(n // tn, m // tm),
        in_specs=[pl.BlockSpec((tm, k_dim), lambda j, i: (i, 0)),
                  pl.BlockSpec((k_dim, tn), lambda j, i: (0, j)),
                  pl.BlockSpec((1, tn), lambda j, i: (0, j)),
                  pl.BlockSpec((1, tn), lambda j, i: (0, j)),
                  pl.BlockSpec((tm, tn), lambda j, i: (i, j))],
        out_specs=pl.BlockSpec((tm, tn), lambda j, i: (i, j)),
        out_shape=jax.ShapeDtypeStruct((m, n), F32),
        compiler_params=_params(("arbitrary", "arbitrary"), 56),
        name="mm_residual",
    )(a, w, bias.reshape(1, n).astype(F32), gate.reshape(1, n).astype(F32), x)


def _mm_glu_kernel(a_ref, wa_ref, wg_ref, ba_ref, bg_ref, o_ref):
    a = a_ref[...]
    u = _dot(a, wa_ref[...]) + ba_ref[...]
    g = _dot(a, wg_ref[...]) + bg_ref[...]
    o_ref[...] = u * jax.nn.sigmoid(g)


def _mm_glu(a, w, bias):
    m, k_dim = a.shape
    n = w.shape[1] // 2
    tm, tn = min(512, m), min(512, n)
    nj = n // tn
    b2 = bias.reshape(1, 2 * n).astype(F32)
    return pl.pallas_call(
        _mm_glu_kernel,
        grid=(nj, m // tm),
        in_specs=[pl.BlockSpec((tm, k_dim), lambda j, i: (i, 0)),
                  pl.BlockSpec((k_dim, tn), lambda j, i: (0, j)),
                  pl.BlockSpec((k_dim, tn), lambda j, i: (0, j + nj)),
                  pl.BlockSpec((1, tn), lambda j, i: (0, j)),
                  pl.BlockSpec((1, tn), lambda j, i: (0, j + nj))],
        out_specs=pl.BlockSpec((tm, tn), lambda j, i: (i, j)),
        out_shape=jax.ShapeDtypeStruct((m, n), F32),
        compiler_params=_params(("arbitrary", "arbitrary"), 56),
        name="mm_glu",
    )(a, w, w, b2, b2)


def _mm_qkv_kernel(a_ref, w_ref, gs_ref, o_ref, *, n_norm_tiles):
    j = pl.program_id(0)
    acc = _dot(a_ref[...], w_ref[...])
    heads = o_ref.shape[0]

    @pl.when(j < n_norm_tiles)
    def _():
        gain, scale = gs_ref[0, 0:1, :], gs_ref[0, 1:2, :]
        for hh in range(heads):
            seg = acc[:, hh * HEAD_DIM:(hh + 1) * HEAD_DIM]
            ms = jnp.mean(seg * seg, axis=-1, keepdims=True)
            o_ref[hh] = (((seg * lax.rsqrt(ms + EPS)) * gain) * scale).astype(o_ref.dtype)

    @pl.when(j >= n_norm_tiles)
    def _():
        for hh in range(heads):
            o_ref[hh] = acc[:, hh * HEAD_DIM:(hh + 1) * HEAD_DIM].astype(o_ref.dtype)


def _mm_qkv(a, w, q_g, k_g):
    m, k_dim = a.shape
    n3 = w.shape[1]
    d = n3 // 3
    tm, tn = min(512, m), min(1024, d)
    hpt = tn // HEAD_DIM
    nj = n3 // tn
    n_qk = 2 * d // tn
    scale = HEAD_DIM ** -0.5
    rows = []
    for j in range(nj):
        if j < d // tn:
            rows.append(_vec_rows(q_g, jnp.full((HEAD_DIM,), scale, F32)))
        elif j < n_qk:
            rows.append(_vec_rows(k_g, jnp.ones((HEAD_DIM,), F32)))
        else:
            rows.append(_vec_rows(jnp.ones((HEAD_DIM,), F32), jnp.ones((HEAD_DIM,), F32)))
    gs = jnp.stack(rows)
    return pl.pallas_call(
        functools.partial(_mm_qkv_kernel, n_norm_tiles=n_qk),
        grid=(nj, m // tm),
        in_specs=[pl.BlockSpec((tm, k_dim), lambda j, i: (i, 0)),
                  pl.BlockSpec((k_dim, tn), lambda j, i: (0, j)),
                  pl.BlockSpec((1, SUBLANES, HEAD_DIM), lambda j, i: (j, 0, 0))],
        out_specs=pl.BlockSpec((hpt, tm, HEAD_DIM), lambda j, i: (j, i, 0)),
        out_shape=jax.ShapeDtypeStruct((n3 // HEAD_DIM, m, HEAD_DIM), BF16),
        compiler_params=_params(("arbitrary", "arbitrary"), 56),
        name="mm_qkv",
    )(a, w, gs)


def _attn_bias(rpb, n_rows):
    n_heads = rpb.shape[0]
    w, r = GRID_W, Q_ROWS
    nb = n_rows // r
    qc = jnp.arange(w)[:, None]
    kc = jnp.arange(w)[None, :]
    cs = jnp.clip(qc - NA_KW // 2, 0, w - NA_KW)
    col_ok = (kc >= cs) & (kc < cs + NA_KW)
    relc = jnp.clip(kc - qc + NA_KW - 1, 0, 2 * NA_KW - 2)
    col_tbl = jnp.where(col_ok[None, None], rpb[:, :, relc], MASK_VALUE)
    masked = jnp.full((n_heads, w, w), MASK_VALUE, F32)
    variants = []
    for b in (0, 1, nb - 1):
        q_rows = []
        for qr in range(r):
            row = r * b + qr
            rs = min(max(row - NA_KH // 2, 0), n_rows - NA_KH)
            blocks = []
            for p in range(3):
                for kr in range(r):
                    krow = r * (b - 1 + p) + kr
                    if rs <= krow < rs + NA_KH:
                        blocks.append(col_tbl[:, krow - row + NA_KH - 1])
                    else:
                        blocks.append(masked)
            q_rows.append(jnp.concatenate(blocks, axis=-1))
        variants.append(jnp.concatenate(q_rows, axis=-2))
    return jnp.stack(variants).astype(F32)


def _attn_kernel(q_ref, kp_ref, kc_ref, kn_ref, vp_ref, vc_ref, vn_ref, kx_ref, vx_ref, bias_ref, o_ref):
    hb = q_ref.shape[0]
    tq = q_ref.shape[1]
    for h in range(hb):
        q = q_ref[h]
        s_loc = jnp.concatenate([_dot_nt(q, kp_ref[h]), _dot_nt(q, kc_ref[h]), _dot_nt(q, kn_ref[h])], axis=1)
        s_loc = s_loc + bias_ref[0, h]
        s_ctx = _dot_nt(q, kx_ref[h])
        m = jnp.maximum(jnp.max(s_loc, axis=-1, keepdims=True), jnp.max(s_ctx, axis=-1, keepdims=True))
        p_loc = jnp.exp(s_loc - m)
        p_ctx = jnp.exp(s_ctx - m)
        den = jnp.sum(p_loc, axis=-1, keepdims=True) + jnp.sum(p_ctx, axis=-1, keepdims=True)
        pl_b = p_loc.astype(BF16)
        o = (_dot(pl_b[:, 0:tq], vp_ref[h]) + _dot(pl_b[:, tq:2 * tq], vc_ref[h])
             + _dot(pl_b[:, 2 * tq:3 * tq], vn_ref[h]) + _dot(p_ctx.astype(BF16), vx_ref[h]))
        o_ref[:, h * HEAD_DIM:(h + 1) * HEAD_DIM] = (o / den).astype(o_ref.dtype)


def _attention(qkv, kv_ctx, bias, n_heads):
    n = qkv.shape[1]
    lc = kv_ctx.shape[1]
    tq = Q_ROWS * GRID_W
    nb = n // tq
    hb = min(4, n_heads)
    ng = n_heads // hb
    blk = (hb, tq, HEAD_DIM)

    def kv_spec(base, shift):
        return pl.BlockSpec(blk, lambda hg, b: (base + hg, jnp.clip(b + shift, 0, nb - 1), 0))

    def variant(b):
        return jnp.where(b == 0, 0, jnp.where(b == nb - 1, 2, 1))

    return pl.pallas_call(
        _attn_kernel,
        grid=(ng, nb),
        in_specs=[pl.BlockSpec(blk, lambda hg, b: (hg, b, 0)),
                  kv_spec(ng, -1), kv_spec(ng, 0), kv_spec(ng, 1),
                  kv_spec(2 * ng, -1), kv_spec(2 * ng, 0), kv_spec(2 * ng, 1),
                  pl.BlockSpec((hb, lc, HEAD_DIM), lambda hg, b: (ng + hg, 0, 0)),
                  pl.BlockSpec((hb, lc, HEAD_DIM), lambda hg, b: (2 * ng + hg, 0, 0)),
                  pl.BlockSpec((1, hb, tq, 3 * tq), lambda hg, b: (variant(b), hg, 0, 0))],
        out_specs=pl.BlockSpec((tq, hb * HEAD_DIM), lambda hg, b: (b, hg)),
        out_shape=jax.ShapeDtypeStruct((n, n_heads * HEAD_DIM), BF16),
        compiler_params=_params(("arbitrary", "arbitrary"), 56),
        name="na_attention",
    )(qkv, qkv, qkv, qkv, qkv, qkv, qkv, kv_ctx, kv_ctx, bias)


def _conv_kernel(up_ref, u_ref, un_ref, w_ref, v_ref, o_ref, pad_ref, acc_ref, *, width):
    i = pl.program_id(0)
    nt = pl.num_programs(0)
    tm, d = u_ref.shape
    halo = up_ref.shape[0]
    nc, cw = w_ref.shape[0], w_ref.shape[3]
    top = jnp.where(i > 0, up_ref[...], 0.0)
    bot = jnp.where(i < nt - 1, un_ref[...], 0.0)
    for c in range(nc):
        cols = slice(c * cw, (c + 1) * cw)
        pad_ref[c, 0:halo, :] = top[:, cols]
        pad_ref[c, halo:halo + tm, :] = u_ref[:, cols]
        pad_ref[c, halo + tm:, :] = bot[:, cols]
    first = halo - width // 2
    n_groups = tm // SUBLANES
    reach = (first + width - 1) // SUBLANES + 1
    row = lax.broadcasted_iota(jnp.int32, (SUBLANES, cw), 0)

    def chunk(c, carry):
        prev = None
        for g in range(n_groups + 1):
            rows = [pad_ref[c, SUBLANES * (g + m):SUBLANES * (g + m + 1), :]
                    for m in range(reach) if SUBLANES * (g + m + 1) <= tm + 2 * halo]
            cur = []
            for s in range(SUBLANES):
                q = None
                for m, blk in enumerate(rows):
                    k = s + SUBLANES * m - first
                    if 0 <= k < width:
                        term = w_ref[c, k] * blk
                        q = term if q is None else q + term
                cur.append(q)
            if prev is not None:
                out = prev[0]
                for s in range(1, SUBLANES):
                    mixed = jnp.where(row >= s, prev[s], cur[s])
                    out = out + pltpu.roll(mixed, SUBLANES - s, 0)
                acc_ref[c, SUBLANES * (g - 1):SUBLANES * g, :] = out
            prev = cur
        return carry

    lax.fori_loop(0, nc, chunk, 0)
    total = None
    for c in range(nc):
        part = jnp.sum(acc_ref[c] + v_ref[0:1, c * cw:(c + 1) * cw], axis=-1, keepdims=True)
        total = part if total is None else total + part
    mu = total / d
    sq = None
    for c in range(nc):
        cen = acc_ref[c] + v_ref[0:1, c * cw:(c + 1) * cw] - mu
        part = jnp.sum(cen * cen, axis=-1, keepdims=True)
        sq = part if sq is None else sq + part
    inv = lax.rsqrt(sq / d + EPS)
    for c in range(nc):
        cols = slice(c * cw, (c + 1) * cw)
        cen = acc_ref[c] + v_ref[0:1, cols] - mu
        y = cen * inv * v_ref[1:2, cols] + v_ref[2:3, cols]
        o_ref[:, cols] = _silu(y).astype(o_ref.dtype)


def _conv_ln_silu(u, w_dw, b_dw, ln_g, ln_b):
    n, d = u.shape
    width = w_dw.shape[0]
    halo = 2 * SUBLANES
    assert width // 2 <= halo
    tm = min(128, n)
    hb = tm // halo
    nh = n // halo
    cw = LANES
    nc = d // cw
    w_b = jnp.broadcast_to(w_dw.reshape(width, 1, nc, cw), (width, SUBLANES, nc, cw)).transpose(2, 0, 1, 3).astype(F32)
    return pl.pallas_call(
        functools.partial(_conv_kernel, width=width),
        grid=(n // tm,),
        in_specs=[pl.BlockSpec((halo, d), lambda i: (jnp.maximum(i * hb - 1, 0), 0)),
                  pl.BlockSpec((tm, d), lambda i: (i, 0)),
                  pl.BlockSpec((halo, d), lambda i: (jnp.minimum((i + 1) * hb, nh - 1), 0)),
                  pl.BlockSpec((nc, width, SUBLANES, cw), lambda i: (0, 0, 0, 0)),
                  pl.BlockSpec((SUBLANES, d), lambda i: (0, 0))],
        out_specs=pl.BlockSpec((tm, d), lambda i: (i, 0)),
        out_shape=jax.ShapeDtypeStruct((n, d), BF16),
        scratch_shapes=[pltpu.VMEM((nc, tm + 2 * halo, cw), F32), pltpu.VMEM((nc, tm, cw), F32)],
        compiler_params=_params(("arbitrary",), 48),
        name="conv_ln_silu",
    )(u, u, u, w_b, _vec_rows(b_dw, ln_g, ln_b))


def kernel(x, c, ctx, c_ctx, ada_w, ada_b, norm_g, pool_w, pool_scale, na_w_qkv, na_w_o, na_q_g, na_k_g, na_rpb,
           conv_w_pw1, conv_b_pw1, conv_w_dw, conv_b_dw, conv_ln_g, conv_ln_b, conv_w_pw2, conv_b_pw2,
           moe_w_router, moe_w_gate, moe_w_up, moe_w_down):
    depth = ada_w.shape[0]
    batch, n, d = x.shape
    assert batch == 1 and c.shape[0] == 1 and ctx.shape[0] == 1
    n_heads = na_rpb.shape[1]
    assert n_heads * HEAD_DIM == d and n % (Q_ROWS * GRID_W) == 0
    n_mixers = 3
    zeros_d = jnp.zeros((d,), F32)

    mods = _ada_mods(jnp.concatenate([c, c_ctx[None, :]], axis=0), ada_w, ada_b)
    xs = x[0]
    cs = ctx[0]
    for i in range(depth):
        kind, slot = i % n_mixers, i // n_mixers
        ctx_out_live = any(j % n_mixers == 1 for j in range(i + 1, depth))
        ctx_in_live = ctx_out_live or kind == 1
        sh1, sc1, g1, sh2, sc2, g2 = jnp.split(mods[i, 0], 6)
        csh1, csc1, cg1, csh2, csc2, cg2 = jnp.split(mods[i, 1], 6)
        ng1, ng2 = norm_g[i, 0], norm_g[i, 1]
        moe_w = (moe_w_router[i], moe_w_gate[i].astype(BF16), moe_w_up[i].astype(BF16), moe_w_down[i].astype(BF16))

        if kind == 0:
            xs = _pool_layer(xs, ng1, sc1, sh1, g1, pool_w[slot], pool_scale[slot])
            if ctx_out_live:
                cs_mixed = _pool_layer(cs, ng1, csc1, csh1, cg1, pool_w[slot], pool_scale[slot])
        elif kind == 1:
            w_qkv = na_w_qkv[slot].astype(BF16)
            h = _norm_bf16(xs, ng1, sc1, sh1)
            hc = _norm_bf16(cs, ng1, csc1, csh1)
            qkv = _mm_qkv(h, w_qkv, na_q_g[slot], na_k_g[slot])
            kv_ctx = _mm_qkv(hc, w_qkv, na_q_g[slot], na_k_g[slot])
            o = _attention(qkv, kv_ctx, _attn_bias(na_rpb[slot], n // GRID_W), n_heads)
            xs = _mm_residual(o, na_w_o[slot].astype(BF16), zeros_d, g1, xs)
            assert not ctx_out_live
        else:
            h = _norm_bf16(xs, ng1, sc1, sh1)
            u = _mm_glu(h, conv_w_pw1[slot].astype(BF16), conv_b_pw1[slot])
            v = _conv_ln_silu(u, conv_w_dw[slot], conv_b_dw[slot], conv_ln_g[slot], conv_ln_b[slot])
            xs = _mm_residual(v, conv_w_pw2[slot].astype(BF16), conv_b_pw2[slot], g1, xs)
            assert not ctx_out_live

        xs = _moe_layer(xs, ng2, sc2, sh2, g2, *moe_w)
        if ctx_out_live:
            cs = _moe_layer(cs_mixed, ng2, csc2, csh2, cg2, *moe_w)
        del ctx_in_live
    return xs[None]
```

```python
import functools

import jax
import jax.numpy as jnp
from jax import lax
from jax.experimental import pallas as pl
from jax.experimental.pallas import tpu as pltpu

EPS = 1e-6
GRID_W = 64
HEAD_DIM = 128
NA_KH = 8
NA_KW = 16
Q_ROWS = 4
EC_CAPACITY = 2
POOL_WINDOWS = (2, 4, 8, 16)
LANES = 128
SUBLANES = 8
MASK_VALUE = -1e30
F32 = jnp.float32
BF16 = jnp.bfloat16


def _params(sem, vmem_mb):
    return pltpu.CompilerParams(dimension_semantics=sem, vmem_limit_bytes=vmem_mb * 1024 * 1024)


def _norm_mod(x, g, sc, sh):
    ms = jnp.mean(x * x, axis=-1, keepdims=True)
    y = x * lax.rsqrt(ms + EPS)
    return (y * g) * (1.0 + sc) + sh


def _silu(v):
    return v * jax.nn.sigmoid(v)


def _dot(a, b):
    return jnp.dot(a, b, preferred_element_type=F32)


def _dot_nt(a, b):
    return lax.dot_general(a, b, (((1,), (1,)), ((), ())), preferred_element_type=F32)


def _ada_kernel(cb_ref, w_ref, b_ref, o_ref, s_ref):
    k_dim, tn = w_ref.shape[1], w_ref.shape[2]
    reps = tn // LANES

    @pl.when(jnp.logical_and(pl.program_id(0) == 0, pl.program_id(1) == 0))
    def _():
        s_ref[...] = _silu(cb_ref[...])

    def body(kc, accs):
        k0 = pl.multiple_of(kc * SUBLANES, SUBLANES)
        w = w_ref[0, pl.ds(k0, SUBLANES), :]
        return tuple(accs[v] + w * jnp.concatenate([s_ref[v, pl.ds(k0, SUBLANES), :]] * reps, axis=1)
                     for v in range(2))

    zero = jnp.zeros((SUBLANES, tn), F32)
    accs = lax.fori_loop(0, k_dim // SUBLANES, body, (zero, zero), unroll=8)
    rows = [jnp.sum(a, axis=0, keepdims=True) for a in accs]
    o_ref[0] = jnp.concatenate(rows, axis=0) + b_ref[0]


def _ada_mods(c2, ada_w, ada_b):
    n_layers, k_dim, n6 = ada_w.shape
    tn = min(512, n6)
    cb = jnp.broadcast_to(c2[:, :, None], (2, k_dim, LANES))
    return pl.pallas_call(
        _ada_kernel,
        grid=(n_layers, n6 // tn),
        in_specs=[pl.BlockSpec((2, k_dim, LANES), lambda l, j: (0, 0, 0)),
                  pl.BlockSpec((1, k_dim, tn), lambda l, j: (l, 0, j)),
                  pl.BlockSpec((1, 1, tn), lambda l, j: (l, 0, j))],
        out_specs=pl.BlockSpec((1, 2, tn), lambda l, j: (l, 0, j)),
        out_shape=jax.ShapeDtypeStruct((n_layers, 2, n6), F32),
        scratch_shapes=[pltpu.VMEM((2, k_dim, LANES), F32)],
        compiler_params=_params(("arbitrary", "arbitrary"), 48),
        name="ada_mods",
    )(cb, ada_w, ada_b.reshape(n_layers, 1, n6))


def _vec_rows(*vecs):
    d = vecs[0].shape[-1]
    rows = [v.reshape(1, d).astype(F32) for v in vecs]
    rows.append(jnp.zeros((SUBLANES - len(vecs), d), F32))
    return jnp.concatenate(rows, axis=0)


def _norm_kernel(x_ref, v_ref, o_ref):
    o_ref[...] = _norm_mod(x_ref[...], v_ref[0:1, :], v_ref[1:2, :], v_ref[2:3, :]).astype(o_ref.dtype)


def _norm_bf16(x, g, sc, sh):
    n, d = x.shape
    tm = min(256, n)
    return pl.pallas_call(
        _norm_kernel,
        grid=(n // tm,),
        in_specs=[pl.BlockSpec((tm, d), lambda i: (i, 0)),
                  pl.BlockSpec((SUBLANES, d), lambda i: (0, 0))],
        out_specs=pl.BlockSpec((tm, d), lambda i: (i, 0)),
        out_shape=jax.ShapeDtypeStruct((n, d), BF16),
        compiler_params=_params(("arbitrary",), 48),
        name="norm_mod",
    )(x, _vec_rows(g, sc, sh))


def _pool_kernel(xp_ref, x_ref, xn_ref, v_ref, w_ref, o_ref, hp_ref, *, n_total):
    i = pl.program_id(0)
    nt = pl.num_programs(0)
    tm, d = x_ref.shape
    halo = SUBLANES
    g, sc, sh = v_ref[0:1, :], v_ref[1:2, :], v_ref[2:3, :]
    x = x_ref[...]
    h = _norm_mod(x, g, sc, sh)
    hp_ref[halo:halo + tm, :] = h
    hp_ref[0:halo, :] = jnp.where(i > 0, _norm_mod(xp_ref[...], g, sc, sh), 0.0)
    hp_ref[halo + tm:, :] = jnp.where(i < nt - 1, _norm_mod(xn_ref[...], g, sc, sh), 0.0)
    t = i * tm + lax.broadcasted_iota(jnp.int32, (tm, 1), 0)
    dg = d // len(POOL_WINDOWS)
    for gi, win in enumerate(POOL_WINDOWS):
        c0 = gi * dg
        hg = hp_ref[:, c0:c0 + dg]
        total = tm + 2 * halo
        acc = hg + pltpu.roll(hg, 1, 0)
        step = 1
        while 2 * step < win:
            acc = pltpu.roll(acc, step, 0) + pltpu.roll(acc, total - step, 0)
            step *= 2
        acc = acc[halo:halo + tm, :]
        cnt = jnp.minimum(t - win // 2 + win, n_total) - jnp.maximum(t - win // 2, 0)
        diff = acc / cnt.astype(F32) - h[:, c0:c0 + dg]
        y = _dot(diff.astype(BF16), w_ref[gi]) * v_ref[4:5, c0:c0 + dg]
        o_ref[:, c0:c0 + dg] = x[:, c0:c0 + dg] + v_ref[3:4, c0:c0 + dg] * y


def _pool_layer(x, g, sc, sh, gate, w_grp, pool_scale):
    n, d = x.shape
    tm = min(256, n)
    hb = tm // SUBLANES
    nb8 = n // SUBLANES
    n_groups, dg, _ = w_grp.shape
    return pl.pallas_call(
        functools.partial(_pool_kernel, n_total=n),
        grid=(n // tm,),
        in_specs=[pl.BlockSpec((SUBLANES, d), lambda i: (jnp.maximum(i * hb - 1, 0), 0)),
                  pl.BlockSpec((tm, d), lambda i: (i, 0)),
                  pl.BlockSpec((SUBLANES, d), lambda i: (jnp.minimum((i + 1) * hb, nb8 - 1), 0)),
                  pl.BlockSpec((SUBLANES, d), lambda i: (0, 0)),
                  pl.BlockSpec((n_groups, dg, dg), lambda i: (0, 0, 0))],
        out_specs=pl.BlockSpec((tm, d), lambda i: (i, 0)),
        out_shape=jax.ShapeDtypeStruct((n, d), F32),
        scratch_shapes=[pltpu.VMEM((tm + 2 * SUBLANES, d), F32)],
        compiler_params=_params(("arbitrary",), 56),
        name="pool_layer",
    )(x, x, x, _vec_rows(g, sc, sh, gate, pool_scale), w_grp.astype(BF16))


def _router_kernel(x_ref, v_ref, whi_ref, wlo_ref, aff_ref, hp_ref, *, n_experts):
    h = _norm_mod(x_ref[...], v_ref[0:1, :], v_ref[1:2, :], v_ref[2:3, :])
    h_b = h.astype(BF16)
    h_res = (h - h_b.astype(F32)).astype(BF16)
    logits = _dot(h_b, whi_ref[...]) + (_dot(h_b, wlo_ref[...]) + _dot(h_res, whi_ref[...]))
    lt = logits.T[0:n_experts, :]
    m = jnp.max(lt, axis=0, keepdims=True)
    p = jnp.exp(lt - m)
    aff_ref[...] = p / jnp.sum(p, axis=0, keepdims=True)
    half = h.shape[1] // 2
    rounded = lax.bitcast_convert_type(h_b.astype(F32), jnp.uint32)
    hp_ref[...] = (rounded[:, half:] & jnp.uint32(0xFFFF0000)) | (rounded[:, :half] >> 16)


def _router(x, g, sc, sh, w_router):
    n, d = x.shape
    n_experts = w_router.shape[1]
    tm = min(256, n)
    wpad = jnp.zeros((d, LANES), F32).at[:, :n_experts].set(w_router)
    w_hi = wpad.astype(BF16)
    w_lo = (wpad - w_hi.astype(F32)).astype(BF16)
    return pl.pallas_call(
        functools.partial(_router_kernel, n_experts=n_experts),
        grid=(n // tm,),
        in_specs=[pl.BlockSpec((tm, d), lambda i: (i, 0)),
                  pl.BlockSpec((SUBLANES, d), lambda i: (0, 0)),
                  pl.BlockSpec((d, LANES), lambda i: (0, 0)),
                  pl.BlockSpec((d, LANES), lambda i: (0, 0))],
        out_specs=[pl.BlockSpec((n_experts, tm), lambda i: (0, i)),
                   pl.BlockSpec((tm, d // 2), lambda i: (i, 0))],
        out_shape=[jax.ShapeDtypeStruct((n_experts, n), F32),
                   jax.ShapeDtypeStruct((n, d // 2), jnp.uint32)],
        compiler_params=_params(("arbitrary",), 48),
        name="router",
    )(x, _vec_rows(g, sc, sh), w_hi, w_lo)


def _sum_all(v):
    return jnp.sum(jnp.sum(v, axis=0, keepdims=True), axis=1, keepdims=True)


def _topk_kernel(aff_ref, idx_ref, gate_ref, thr_ref, *, cap):
    n_experts, n_rows, _ = aff_ref.shape
    cap_pad = idx_ref.shape[1]

    bits = lax.bitcast_convert_type(aff_ref[...], jnp.int32)
    thr = jnp.zeros((n_experts, 1, 1), jnp.int32)
    for b in range(30, -1, -1):
        cand = thr | (1 << b)
        hit = jnp.where(bits >= cand, 1.0, 0.0)
        cnt = jnp.sum(jnp.sum(hit, axis=1, keepdims=True), axis=2, keepdims=True)
        thr = jnp.where(cnt >= cap, cand, thr)
    thr_ref[...] = jnp.broadcast_to(thr, thr_ref.shape)

    def tri(shape, strict):
        a = lax.broadcasted_iota(jnp.int32, shape, 0)
        b = lax.broadcasted_iota(jnp.int32, shape, 1)
        return jnp.where(a < b if strict else a <= b, 1.0, 0.0).astype(BF16)

    lane_incl = tri((LANES, LANES), False)
    row_incl = tri((n_rows, n_rows), False)
    row_before = tri((n_rows, n_rows), True).T
    ones8 = jnp.ones((SUBLANES, LANES), BF16)
    slot = lax.broadcasted_iota(jnp.int32, (cap_pad, n_rows), 0).astype(F32)
    ridx = lax.broadcasted_iota(jnp.int32, (cap_pad, n_rows), 1).astype(F32)
    lane = lax.broadcasted_iota(jnp.int32, (cap_pad, LANES), 1).astype(F32)

    def expert(e, carry):
        aff = aff_ref[e]
        bt = lax.bitcast_convert_type(aff, jnp.int32)
        te = thr_ref[e][0:1, :]
        gt = bt > te
        eq = bt == te
        need = cap - _sum_all(jnp.where(gt, 1.0, 0.0))
        eq_f = jnp.where(eq, 1.0, 0.0)
        p_eq = _dot(eq_f.astype(BF16), lane_incl)
        tot_eq = jnp.broadcast_to(p_eq[:, LANES - 1:LANES], (n_rows, LANES)).astype(BF16)
        rank_eq = p_eq - eq_f + _dot(row_before, tot_eq)
        sel = jnp.logical_or(gt, jnp.logical_and(eq, rank_eq < need))
        sel_b = jnp.where(sel, 1.0, 0.0).astype(BF16)
        p_in = _dot(sel_b, lane_incl)
        tot_row = _dot_nt(ones8, sel_b)
        incl_row = _dot(tot_row.astype(BF16), row_incl)
        lo = (incl_row - tot_row)[0:1, :]
        hi = incl_row[0:1, :]
        onehot = jnp.where(jnp.logical_and(lo <= slot, slot < hi), 1.0, 0.0)
        r_star = jnp.sum(onehot * ridx, axis=1, keepdims=True)
        base = jnp.sum(onehot * lo, axis=1, keepdims=True)
        onehot_b = onehot.astype(BF16)
        p_row = _dot(onehot_b, p_in.astype(BF16))
        s_loc = slot[:, 0:1] - base
        l_star = jnp.sum(jnp.where(p_row <= s_loc, 1.0, 0.0), axis=1, keepdims=True)
        idx_ref[e] = (LANES * r_star + l_star).astype(jnp.int32)
        a1 = aff.astype(BF16)
        r1 = aff - a1.astype(F32)
        a2 = r1.astype(BF16)
        a3 = (r1 - a2.astype(F32)).astype(BF16)
        aff_row = (_dot(onehot_b, a1) + _dot(onehot_b, a2)) + _dot(onehot_b, a3)
        gate_ref[e] = jnp.sum(jnp.where(lane == l_star, aff_row, 0.0), axis=1, keepdims=True)
        return carry

    lax.fori_loop(0, n_experts, expert, 0)


def _topk(aff_t, cap):
    n_experts, n = aff_t.shape
    chunk = LANES * LANES
    n_pad = pl.cdiv(n, chunk) * chunk
    cap_pad = pl.cdiv(cap, LANES) * LANES
    aff3 = jnp.pad(aff_t, ((0, 0), (0, n_pad - n)), constant_values=-1.0).reshape(n_experts, n_pad // LANES, LANES)
    idx, gates = pl.pallas_call(
        functools.partial(_topk_kernel, cap=cap),
        out_shape=[jax.ShapeDtypeStruct((n_experts, cap_pad, 1), jnp.int32),
                   jax.ShapeDtypeStruct((n_experts, cap_pad, 1), F32)],
        scratch_shapes=[pltpu.VMEM((n_experts, SUBLANES, LANES), jnp.int32)],
        compiler_params=_params(None, 48),
        name="expert_topk",
    )(aff3)
    return idx[:, :cap, 0], gates[:, :cap]


def _moe_kernel(idx_ref, g_ref, v_ref, wg_ref, wu_ref, wd_ref, hp_hbm, x_hbm, o_hbm,
                hbuf, xbuf, sem_h, sem_x, sem_o, pend):
    del x_hbm
    e, s = pl.program_id(0), pl.program_id(1)
    ns, ne = pl.num_programs(1), pl.num_programs(0)
    st = e * ns + s
    last = ne * ns - 1
    slot = lax.rem(st, 2)
    other = 1 - slot
    groups = hbuf.shape[1]
    ts = groups * SUBLANES
    half = hbuf.shape[3]
    d = xbuf.shape[3]

    def gather(src, dst, sem, first):
        for r in range(ts):
            row = idx_ref[first + r]
            pltpu.make_async_copy(src.at[row >> 3, pl.ds(row & 7, 1)],
                                  dst.at[r // SUBLANES, pl.ds(r % SUBLANES, 1)], sem).start()

    def scatter(src, sem, first):
        for r in range(ts):
            row = idx_ref[first + r]
            pltpu.make_async_copy(src.at[r // SUBLANES, pl.ds(r % SUBLANES, 1)],
                                  o_hbm.at[row >> 3, pl.ds(row & 7, 1)], sem).start()

    def wait_rows(hbm, buf, sem):
        pltpu.make_async_copy(hbm.at[pl.ds(0, groups)], buf, sem).wait()

    @pl.when(st == 0)
    def _():
        pend[0] = 0
        pend[1] = 0
        gather(hp_hbm, hbuf.at[0], sem_h.at[0], 0)

    @pl.when(pend[slot] == 1)
    def _():
        wait_rows(o_hbm, xbuf.at[slot], sem_o.at[slot])
        pend[slot] = 0

    @pl.when(jnp.logical_and(s == 0, pend[other] == 1))
    def _():
        wait_rows(o_hbm, xbuf.at[other], sem_o.at[other])
        pend[other] = 0

    wait_rows(hp_hbm, hbuf.at[slot], sem_h.at[slot])
    gather(o_hbm, xbuf.at[slot], sem_x.at[slot], st * ts)
    gather(hp_hbm, hbuf.at[other], sem_h.at[other], jnp.minimum(st + 1, last) * ts)

    packed = hbuf[slot].reshape(ts, half)
    h_lo = lax.bitcast_convert_type(packed << 16, F32).astype(BF16)
    h_hi = lax.bitcast_convert_type(packed & jnp.uint32(0xFFFF0000), F32).astype(BF16)
    a = _dot(h_lo, wg_ref[0, 0:half, :]) + _dot(h_hi, wg_ref[0, half:, :])
    b = _dot(h_lo, wu_ref[0, 0:half, :]) + _dot(h_hi, wu_ref[0, half:, :])
    hid = (_silu(a) * b).astype(BF16)
    y = _dot(hid, wd_ref[0]) * g_ref[0]
    wait_rows(o_hbm, xbuf.at[slot], sem_x.at[slot])
    xbuf[slot] = xbuf[slot] + (v_ref[0:1, :] * y).reshape(groups, SUBLANES, d)
    scatter(xbuf.at[slot], sem_o.at[slot], st * ts)
    pend[slot] = 1

    @pl.when(st == last)
    def _():
        wait_rows(hp_hbm, hbuf.at[other], sem_h.at[other])
        wait_rows(o_hbm, xbuf.at[slot], sem_o.at[slot])
        pend[slot] = 0

        @pl.when(pend[other] == 1)
        def _():
            wait_rows(o_hbm, xbuf.at[other], sem_o.at[other])
            pend[other] = 0


def _moe_apply(x, h_packed, idx, gates, gate2, w_gate, w_up, w_down):
    n, d = x.shape
    n_experts, cap = idx.shape
    f = w_gate.shape[-1]
    ts = min(256, cap)
    grid_spec = pltpu.PrefetchScalarGridSpec(
        num_scalar_prefetch=1,
        grid=(n_experts, cap // ts),
        in_specs=[pl.BlockSpec((1, ts, 1), lambda e, s, idx_ref: (e, s, 0)),
                  pl.BlockSpec((SUBLANES, d), lambda e, s, idx_ref: (0, 0)),
                  pl.BlockSpec((1, d, f), lambda e, s, idx_ref: (e, 0, 0)),
                  pl.BlockSpec((1, d, f), lambda e, s, idx_ref: (e, 0, 0)),
                  pl.BlockSpec((1, f, d), lambda e, s, idx_ref: (e, 0, 0)),
                  pl.BlockSpec(memory_space=pl.ANY),
                  pl.BlockSpec(memory_space=pl.ANY)],
        out_specs=pl.BlockSpec(memory_space=pl.ANY),
        scratch_shapes=[pltpu.VMEM((2, ts // SUBLANES, SUBLANES, d // 2), jnp.uint32),
                        pltpu.VMEM((2, ts // SUBLANES, SUBLANES, d), F32),
                        pltpu.SemaphoreType.DMA((2,)),
                        pltpu.SemaphoreType.DMA((2,)),
                        pltpu.SemaphoreType.DMA((2,)),
                        pltpu.SMEM((2,), jnp.int32)])
    out = pl.pallas_call(
        _moe_kernel,
        grid_spec=grid_spec,
        out_shape=jax.ShapeDtypeStruct((n // SUBLANES, SUBLANES, d), F32),
        input_output_aliases={7: 0},
        compiler_params=_params(("arbitrary", "arbitrary"), 56),
        name="moe_experts",
    )(idx.reshape(-1), gates, _vec_rows(gate2), w_gate, w_up, w_down,
      h_packed.reshape(n // SUBLANES, SUBLANES, d // 2), x.reshape(n // SUBLANES, SUBLANES, d))
    return out.reshape(n, d)


def _moe_layer(x, g, sc, sh, gate2, w_router, w_gate, w_up, w_down):
    n = x.shape[0]
    n_experts = w_router.shape[1]
    cap = max(1, EC_CAPACITY * n // n_experts)
    aff_t, h_packed = _router(x, g, sc, sh, w_router)
    idx, gates = _topk(aff_t, cap)
    return _moe_apply(x, h_packed, idx, gates, gate2, w_gate, w_up, w_down)


def _mm_residual_kernel(a_ref, w_ref, b_ref, gate_ref, x_ref, o_ref):
    y = _dot(a_ref[...], w_ref[...]) + b_ref[...]
    o_ref[...] = x_ref[...] + gate_ref[...] * y


def _mm_residual(a, w, bias, gate, x):
    m, k_dim = a.shape
    n = w.shape[1]
    tm, tn = min(512, m), min(1024, n)
    return pl.pallas_call(
        _mm_residual_kernel,
        grid=(n // tn, m // tm),
        in_specs=[pl.BlockSpec((tm, k_dim), lambda j, i: (i, 0)),
                  pl.BlockSpec((k_dim, tn), lambda j, i: (0, j)),
                  pl.BlockSpec((1, tn), lambda j, i: (0, j)),
                  pl.BlockSpec((1, tn), lambda j, i: (0, j)),
                  pl.BlockSpec((tm, tn), lambda j, i: (i, j))],
        out_specs=pl.BlockSpec((tm, tn), lambda j, i: (i, j)),
        out_shape=jax.ShapeDtypeStruct((m, n), F32),
        compiler_params=_params(("arbitrary", "arbitrary"), 56),
        name="mm_residual",
    )(a, w, bias.reshape(1, n).astype(F32), gate.reshape(1, n).astype(F32), x)


def _mm_glu_kernel(a_ref, wa_ref, wg_ref, ba_ref, bg_ref, o_ref):
    a = a_ref[...]
    u = _dot(a, wa_ref[...]) + ba_ref[...]
    g = _dot(a, wg_ref[...]) + bg_ref[...]
    o_ref[...] = u * jax.nn.sigmoid(g)


def _mm_glu(a, w, bias):
    m, k_dim = a.shape
    n = w.shape[1] // 2
    tm, tn = min(512, m), min(512, n)
    nj = n // tn
    b2 = bias.reshape(1, 2 * n).astype(F32)
    return pl.pallas_call(
        _mm_glu_kernel,
        grid=(nj, m // tm),
        in_specs=[pl.BlockSpec((tm, k_dim), lambda j, i: (i, 0)),
                  pl.BlockSpec((k_dim, tn), lambda j, i: (0, j)),
                  pl.BlockSpec((k_dim, tn), lambda j, i: (0, j + nj)),
                  pl.BlockSpec((1, tn), lambda j, i: (0, j)),
                  pl.BlockSpec((1, tn), lambda j, i: (0, j + nj))],
        out_specs=pl.BlockSpec((tm, tn), lambda j, i: (i, j)),
        out_shape=jax.ShapeDtypeStruct((m, n), F32),
        compiler_params=_params(("arbitrary", "arbitrary"), 56),
        name="mm_glu",
    )(a, w, w, b2, b2)


def _mm_qkv_kernel(a_ref, w_ref, gs_ref, o_ref, *, n_norm_tiles):
    j = pl.program_id(0)
    acc = _dot(a_ref[...], w_ref[...])
    heads = o_ref.shape[0]

    @pl.when(j < n_norm_tiles)
    def _():
        gain, scale = gs_ref[0, 0:1, :], gs_ref[0, 1:2, :]
        for hh in range(heads):
            seg = acc[:, hh * HEAD_DIM:(hh + 1) * HEAD_DIM]
            ms = jnp.mean(seg * seg, axis=-1, keepdims=True)
            o_ref[hh] = (((seg * lax.rsqrt(ms + EPS)) * gain) * scale).astype(o_ref.dtype)

    @pl.when(j >= n_norm_tiles)
    def _():
        for hh in range(heads):
            o_ref[hh] = acc[:, hh * HEAD_DIM:(hh + 1) * HEAD_DIM].astype(o_ref.dtype)


def _mm_qkv(a, w, q_g, k_g):
    m, k_dim = a.shape
    n3 = w.shape[1]
    d = n3 // 3
    tm, tn = min(512, m), min(1024, d)
    hpt = tn // HEAD_DIM
    nj = n3 // tn
    n_qk = 2 * d // tn
    scale = HEAD_DIM ** -0.5
    rows = []
    for j in range(nj):
        if j < d // tn:
            rows.append(_vec_rows(q_g, jnp.full((HEAD_DIM,), scale, F32)))
        elif j < n_qk:
            rows.append(_vec_rows(k_g, jnp.ones((HEAD_DIM,), F32)))
        else:
            rows.append(_vec_rows(jnp.ones((HEAD_DIM,), F32), jnp.ones((HEAD_DIM,), F32)))
    gs = jnp.stack(rows)
    return pl.pallas_call(
        functools.partial(_mm_qkv_kernel, n_norm_tiles=n_qk),
        grid=(nj, m // tm),
        in_specs=[pl.BlockSpec((tm, k_dim), lambda j, i: (i, 0)),
                  pl.BlockSpec((k_dim, tn), lambda j, i: (0, j)),
                  pl.BlockSpec((1, SUBLANES, HEAD_DIM), lambda j, i: (j, 0, 0))],
        out_specs=pl.BlockSpec((hpt, tm, HEAD_DIM), lambda j, i: (j, i, 0)),
        out_shape=jax.ShapeDtypeStruct((n3 // HEAD_DIM, m, HEAD_DIM), BF16),
        compiler_params=_params(("arbitrary", "arbitrary"), 56),
        name="mm_qkv",
    )(a, w, gs)


def _attn_bias(rpb, n_rows):
    n_heads = rpb.shape[0]
    w, r = GRID_W, Q_ROWS
    nb = n_rows // r
    qc = jnp.arange(w)[:, None]
    kc = jnp.arange(w)[None, :]
    cs = jnp.clip(qc - NA_KW // 2, 0, w - NA_KW)
    col_ok = (kc >= cs) & (kc < cs + NA_KW)
    relc = jnp.clip(kc - qc + NA_KW - 1, 0, 2 * NA_KW - 2)
    col_tbl = jnp.where(col_ok[None, None], rpb[:, :, relc], MASK_VALUE)
    masked = jnp.full((n_heads, w, w), MASK_VALUE, F32)
    variants = []
    for b in (0, 1, nb - 1):
        q_rows = []
        for qr in range(r):
            row = r * b + qr
            rs = min(max(row - NA_KH // 2, 0), n_rows - NA_KH)
            blocks = []
            for p in range(3):
                for kr in range(r):
                    krow = r * (b - 1 + p) + kr
                    if rs <= krow < rs + NA_KH:
                        blocks.append(col_tbl[:, krow - row + NA_KH - 1])
                    else:
                        blocks.append(masked)
            q_rows.append(jnp.concatenate(blocks, axis=-1))
        variants.append(jnp.concatenate(q_rows, axis=-2))
    return jnp.stack(variants).astype(F32)


def _attn_kernel(q_ref, kp_ref, kc_ref, kn_ref, vp_ref, vc_ref, vn_ref, kx_ref, vx_ref, bias_ref, o_ref):
    hb = q_ref.shape[0]
    tq = q_ref.shape[1]
    for h in range(hb):
        q = q_ref[h]
        s_loc = jnp.concatenate([_dot_nt(q, kp_ref[h]), _dot_nt(q, kc_ref[h]), _dot_nt(q, kn_ref[h])], axis=1)
        s_loc = s_loc + bias_ref[0, h]
        s_ctx = _dot_nt(q, kx_ref[h])
        m = jnp.maximum(jnp.max(s_loc, axis=-1, keepdims=True), jnp.max(s_ctx, axis=-1, keepdims=True))
        p_loc = jnp.exp(s_loc - m)
        p_ctx = jnp.exp(s_ctx - m)
        den = jnp.sum(p_loc, axis=-1, keepdims=True) + jnp.sum(p_ctx, axis=-1, keepdims=True)
        pl_b = p_loc.astype(BF16)
        o = (_dot(pl_b[:, 0:tq], vp_ref[h]) + _dot(pl_b[:, tq:2 * tq], vc_ref[h])
             + _dot(pl_b[:, 2 * tq:3 * tq], vn_ref[h]) + _dot(p_ctx.astype(BF16), vx_ref[h]))
        o_ref[:, h * HEAD_DIM:(h + 1) * HEAD_DIM] = (o / den).astype(o_ref.dtype)


def _attention(qkv, kv_ctx, bias, n_heads):
    n = qkv.shape[1]
    lc = kv_ctx.shape[1]
    tq = Q_ROWS * GRID_W
    nb = n // tq
    hb = min(4, n_heads)
    ng = n_heads // hb
    blk = (hb, tq, HEAD_DIM)

    def kv_spec(base, shift):
        return pl.BlockSpec(blk, lambda hg, b: (base + hg, jnp.clip(b + shift, 0, nb - 1), 0))

    def variant(b):
        return jnp.where(b == 0, 0, jnp.where(b == nb - 1, 2, 1))

    return pl.pallas_call(
        _attn_kernel,
        grid=(ng, nb),
        in_specs=[pl.BlockSpec(blk, lambda hg, b: (hg, b, 0)),
                  kv_spec(ng, -1), kv_spec(ng, 0), kv_spec(ng, 1),
                  kv_spec(2 * ng, -1), kv_spec(2 * ng, 0), kv_spec(2 * ng, 1),
                  pl.BlockSpec((hb, lc, HEAD_DIM), lambda hg, b: (ng + hg, 0, 0)),
                  pl.BlockSpec((hb, lc, HEAD_DIM), lambda hg, b: (2 * ng + hg, 0, 0)),
                  pl.BlockSpec((1, hb, tq, 3 * tq), lambda hg, b: (variant(b), hg, 0, 0))],
        out_specs=pl.BlockSpec((tq, hb * HEAD_DIM), lambda hg, b: (b, hg)),
        out_shape=jax.ShapeDtypeStruct((n, n_heads * HEAD_DIM), BF16),
        compiler_params=_params(("arbitrary", "arbitrary"), 56),
        name="na_attention",
    )(qkv, qkv, qkv, qkv, qkv, qkv, qkv, kv_ctx, kv_ctx, bias)


def _conv_kernel(up_ref, u_ref, un_ref, w_ref, v_ref, o_ref, pad_ref, acc_ref, *, width):
    i = pl.program_id(0)
    nt = pl.num_programs(0)
    tm, d = u_ref.shape
    halo = up_ref.shape[0]
    nc, cw = w_ref.shape[0], w_ref.shape[3]
    top = jnp.where(i > 0, up_ref[...], 0.0)
    bot = jnp.where(i < nt - 1, un_ref[...], 0.0)
    for c in range(nc):
        cols = slice(c * cw, (c + 1) * cw)
        pad_ref[c, 0:halo, :] = top[:, cols]
        pad_ref[c, halo:halo + tm, :] = u_ref[:, cols]
        pad_ref[c, halo + tm:, :] = bot[:, cols]
    first = halo - width // 2
    n_groups = tm // SUBLANES
    reach = (first + width - 1) // SUBLANES + 1
    row = lax.broadcasted_iota(jnp.int32, (SUBLANES, cw), 0)

    def chunk(c, carry):
        prev = None
        for g in range(n_groups + 1):
            rows = [pad_ref[c, SUBLANES * (g + m):SUBLANES * (g + m + 1), :]
                    for m in range(reach) if SUBLANES * (g + m + 1) <= tm + 2 * halo]
            cur = []
            for s in range(SUBLANES):
                q = None
                for m, blk in enumerate(rows):
                    k = s + SUBLANES * m - first
                    if 0 <= k < width:
                        term = w_ref[c, k] * blk
                        q = term if q is None else q + term
                cur.append(q)
            if prev is not None:
                out = prev[0]
                for s in range(1, SUBLANES):
                    mixed = jnp.where(row >= s, prev[s], cur[s])
                    out = out + pltpu.roll(mixed, SUBLANES - s, 0)
                acc_ref[c, SUBLANES * (g - 1):SUBLANES * g, :] = out
            prev = cur
        return carry

    lax.fori_loop(0, nc, chunk, 0)
    total = None
    for c in range(nc):
        part = jnp.sum(acc_ref[c] + v_ref[0:1, c * cw:(c + 1) * cw], axis=-1, keepdims=True)
        total = part if total is None else total + part
    mu = total / d
    sq = None
    for c in range(nc):
        cen = acc_ref[c] + v_ref[0:1, c * cw:(c + 1) * cw] - mu
        part = jnp.sum(cen * cen, axis=-1, keepdims=True)
        sq = part if sq is None else sq + part
    inv = lax.rsqrt(sq / d + EPS)
    for c in range(nc):
        cols = slice(c * cw, (c + 1) * cw)
        cen = acc_ref[c] + v_ref[0:1, cols] - mu
        y = cen * inv * v_ref[1:2, cols] + v_ref[2:3, cols]
        o_ref[:, cols] = _silu(y).astype(o_ref.dtype)


def _conv_ln_silu(u, w_dw, b_dw, ln_g, ln_b):
    n, d = u.shape
    width = w_dw.shape[0]
    halo = 2 * SUBLANES
    assert width // 2 <= halo
    tm = min(128, n)
    hb = tm // halo
    nh = n // halo
    cw = LANES
    nc = d // cw
    w_b = jnp.broadcast_to(w_dw.reshape(width, 1, nc, cw), (width, SUBLANES, nc, cw)).transpose(2, 0, 1, 3).astype(F32)
    return pl.pallas_call(
        functools.partial(_conv_kernel, width=width),
        grid=(n // tm,),
        in_specs=[pl.BlockSpec((halo, d), lambda i: (jnp.maximum(i * hb - 1, 0), 0)),
                  pl.BlockSpec((tm, d), lambda i: (i, 0)),
                  pl.BlockSpec((halo, d), lambda i: (jnp.minimum((i + 1) * hb, nh - 1), 0)),
                  pl.BlockSpec((nc, width, SUBLANES, cw), lambda i: (0, 0, 0, 0)),
                  pl.BlockSpec((SUBLANES, d), lambda i: (0, 0))],
        out_specs=pl.BlockSpec((tm, d), lambda i: (i, 0)),
        out_shape=jax.ShapeDtypeStruct((n, d), BF16),
        scratch_shapes=[pltpu.VMEM((nc, tm + 2 * halo, cw), F32), pltpu.VMEM((nc, tm, cw), F32)],
        compiler_params=_params(("arbitrary",), 48),
        name="conv_ln_silu",
    )(u, u, u, w_b, _vec_rows(b_dw, ln_g, ln_b))


def kernel(x, c, ctx, c_ctx, ada_w, ada_b, norm_g, pool_w, pool_scale, na_w_qkv, na_w_o, na_q_g, na_k_g, na_rpb,
           conv_w_pw1, conv_b_pw1, conv_w_dw, conv_b_dw, conv_ln_g, conv_ln_b, conv_w_pw2, conv_b_pw2,
           moe_w_router, moe_w_gate, moe_w_up, moe_w_down):
    depth = ada_w.shape[0]
    batch, n, d = x.shape
    assert batch == 1 and c.shape[0] == 1 and ctx.shape[0] == 1
    n_heads = na_rpb.shape[1]
    assert n_heads * HEAD_DIM == d and n % (Q_ROWS * GRID_W) == 0
    n_mixers = 3
    zeros_d = jnp.zeros((d,), F32)

    mods = _ada_mods(jnp.concatenate([c, c_ctx[None, :]], axis=0), ada_w, ada_b)
    xs = x[0]
    cs = ctx[0]
    for i in range(depth):
        kind, slot = i % n_mixers, i // n_mixers
        ctx_out_live = any(j % n_mixers == 1 for j in range(i + 1, depth))
        ctx_in_live = ctx_out_live or kind == 1
        sh1, sc1, g1, sh2, sc2, g2 = jnp.split(mods[i, 0], 6)
        csh1, csc1, cg1, csh2, csc2, cg2 = jnp.split(mods[i, 1], 6)
        ng1, ng2 = norm_g[i, 0], norm_g[i, 1]
        moe_w = (moe_w_router[i], moe_w_gate[i].astype(BF16), moe_w_up[i].astype(BF16), moe_w_down[i].astype(BF16))

        if kind == 0:
            xs = _pool_layer(xs, ng1, sc1, sh1, g1, pool_w[slot], pool_scale[slot])
            if ctx_out_live:
                cs_mixed = _pool_layer(cs, ng1, csc1, csh1, cg1, pool_w[slot], pool_scale[slot])
        elif kind == 1:
            w_qkv = na_w_qkv[slot].astype(BF16)
            h = _norm_bf16(xs, ng1, sc1, sh1)
            hc = _norm_bf16(cs, ng1, csc1, csh1)
            qkv = _mm_qkv(h, w_qkv, na_q_g[slot], na_k_g[slot])
            kv_ctx = _mm_qkv(hc, w_qkv, na_q_g[slot], na_k_g[slot])
            o = _attention(qkv, kv_ctx, _attn_bias(na_rpb[slot], n // GRID_W), n_heads)
            xs = _mm_residual(o, na_w_o[slot].astype(BF16), zeros_d, g1, xs)
            assert not ctx_out_live
        else:
            h = _norm_bf16(xs, ng1, sc1, sh1)
            u = _mm_glu(h, conv_w_pw1[slot].astype(BF16), conv_b_pw1[slot])
            v = _conv_ln_silu(u, conv_w_dw[slot], conv_b_dw[slot], conv_ln_g[slot], conv_ln_b[slot])
            xs = _mm_residual(v, conv_w_pw2[slot].astype(BF16), conv_b_pw2[slot], g1, xs)
            assert not ctx_out_live

        xs = _moe_layer(xs, ng2, sc2, sh2, g2, *moe_w)
        if ctx_out_live:
            cs = _moe_layer(cs_mixed, ng2, csc2, csh2, cg2, *moe_w)
        del ctx_in_live
    return xs[None]
```
